```python
import math
import jax
import jax.numpy as jnp
from jax import lax
import numpy as np

D_MODEL = 1024
BATCH = 2
SEQ = 8192
DEPTH = 2
DEC_BATCH = 8
DEC_SEQ = 64
PAST_LEN = 4096

CHUNK = 64
EPS = 1e-6
D_PLE = 256
D_FF = 3072
FFN_CONV = 3
D_LRU = 512
LRU_HEADS = 8
LRU_HEAD_DIM = D_LRU // LRU_HEADS
LRU_CONV = 4
LRU_C = 8.0
GLA_HEADS = 4
GLA_DK = 64
GLA_DV = 128
GLA_KEY = GLA_HEADS * GLA_DK
GLA_VAL = GLA_HEADS * GLA_DV
GLA_RANK = 16
GLA_TAU = 16.0
SSD_HEADS = 8
SSD_HEAD_DIM = 64
SSD_INNER = SSD_HEADS * SSD_HEAD_DIM
SSD_GROUPS = 2
SSD_STATE = 128
SSD_CONV = 4
SSD_XBC = SSD_INNER + 2 * SSD_GROUPS * SSD_STATE
D_MIX = D_LRU + GLA_VAL + SSD_INNER
IN_WIDTHS = (D_LRU, D_LRU, GLA_KEY, GLA_KEY, GLA_VAL, GLA_VAL, GLA_RANK, SSD_INNER, SSD_XBC, SSD_HEADS)
D_IN = D_LRU * 2 + GLA_KEY * 2 + GLA_VAL * 2 + GLA_RANK + SSD_INNER + SSD_XBC + SSD_HEADS

kernel_name = "hybrid_streaming_encoder_step"


def rmsnorm(x, g):
    xf = x.astype(jnp.float32)
    y = xf * lax.rsqrt(jnp.mean(xf * xf, axis=-1, keepdims=True) + EPS)
    return (y * g.astype(jnp.float32)).astype(x.dtype)


def causal_dwconv(x, buf, w, b):
    T = x.shape[1]
    width = w.shape[0]
    xp = jnp.concatenate([buf.astype(x.dtype), x], axis=1)
    y = b
    for j in range(width):
        y = y + xp[:, j:j + T] * w[j]
    return y, xp[:, T:].astype(buf.dtype)


def _linear_combine(left, right):
    a1, b1 = left
    a2, b2 = right
    return a1 * a2, a2 * b1 + b2


def rglru_mixer(xb, gb, conv_buf, h0, conv_w, conv_b, w_r, b_r, w_i, b_i, lam):
    B, T, _ = xb.shape
    f32 = jnp.float32
    xc, new_buf = causal_dwconv(xb, conv_buf, conv_w, conv_b)
    xh = xc.reshape(B, T, LRU_HEADS, LRU_HEAD_DIM)
    r = jax.nn.sigmoid((jnp.einsum('bthi,hij->bthj', xh, w_r).reshape(B, T, D_LRU) + b_r).astype(f32))
    i_g = jax.nn.sigmoid((jnp.einsum('bthi,hij->bthj', xh, w_i).reshape(B, T, D_LRU) + b_i).astype(f32))
    log_a = -LRU_C * jax.nn.softplus(-lam.astype(f32)) * r
    a = jnp.exp(log_a)
    u = jnp.sqrt(-jnp.expm1(2.0 * log_a)) * (i_g * xc.astype(f32))
    u = u.at[:, 0].add(a[:, 0] * h0.astype(f32))
    _, h = lax.associative_scan(_linear_combine, (a, u), axis=1)
    y = h * jax.nn.gelu(gb.astype(f32), approximate=True)
    return y.astype(xb.dtype), new_buf, h[:, -1].astype(h0.dtype)


def gla_mixer(q, k, v, g, lr, S0, w_lr, b_lr, norm_g):
    B, T, _ = q.shape
    L = min(CHUNK, T)
    N = T // L
    f32 = jnp.float32
    log_alpha = jax.nn.log_sigmoid((lr @ w_lr + b_lr).astype(f32)) / GLA_TAU
    shp_k = (B, N, L, GLA_HEADS, GLA_DK)
    qc = q.astype(f32).reshape(shp_k) * (GLA_DK ** -0.5)
    kc = k.astype(f32).reshape(shp_k)
    vc = v.astype(f32).reshape(B, N, L, GLA_HEADS, GLA_DV)
    cum = jnp.cumsum(log_alpha.reshape(shp_k), axis=2)
    last = cum[:, :, -1]
    q_dec = qc * jnp.exp(cum)
    k_inv = kc * jnp.exp(-cum)
    k_end = kc * jnp.exp(last[:, :, None] - cum)
    mask = jnp.tril(jnp.ones((L, L), dtype=bool))
    scores = jnp.where(mask, jnp.einsum('bnihd,bnjhd->bnhij', q_dec, k_inv), 0.0)
    o_intra = jnp.einsum('bnhij,bnjhe->bnihe', scores, vc)
    dS = jnp.einsum('bnjhd,bnjhe->bnhde', k_end, vc)

    def step(S, inp):
        dec, ds = inp
        return dec[..., None] * S + ds, S

    S_last, S_prev = lax.scan(step, S0.astype(f32), (jnp.moveaxis(jnp.exp(last), 1, 0), jnp.moveaxis(dS, 1, 0)))
    S_prev = jnp.moveaxis(S_prev, 0, 1)
    o = o_intra + jnp.einsum('bnihd,bnhde->bnihe', q_dec, S_prev)
    o = o * lax.rsqrt(jnp.mean(o * o, axis=-1, keepdims=True) + EPS)
    o = o.reshape(B, T, GLA_VAL) * norm_g.astype(f32)
    o = o * jax.nn.silu(g.astype(f32))
    return o.astype(q.dtype), S_last.astype(S0.dtype)


def ssd_mixer(z, xbc, dt_raw, conv_buf, S0, conv_w, conv_b, dt_bias, a_log, d_skip, norm_g):
    B, T, _ = z.shape
    L = min(CHUNK, T)
    N = T // L
    f32 = jnp.float32
    xbc_c, new_buf = causal_dwconv(xbc, conv_buf, conv_w, conv_b)
    xbc_c = jax.nn.silu(xbc_c.astype(f32))
    xs, Bm, Cm = jnp.split(xbc_c, [SSD_INNER, SSD_INNER + SSD_GROUPS * SSD_STATE], axis=-1)
    rep = SSD_HEADS // SSD_GROUPS
    xh = xs.reshape(B, N, L, SSD_HEADS, SSD_HEAD_DIM)
    Bh = jnp.repeat(Bm.reshape(B, T, SSD_GROUPS, SSD_STATE), rep, axis=2).reshape(B, N, L, SSD_HEADS, SSD_STATE)
    Ch = jnp.repeat(Cm.reshape(B, T, SSD_GROUPS, SSD_STATE), rep, axis=2).reshape(B, N, L, SSD_HEADS, SSD_STATE)
    dt = jax.nn.softplus(dt_raw.astype(f32) + dt_bias.astype(f32)).reshape(B, N, L, SSD_HEADS)
    dA = dt * -jnp.exp(a_log.astype(f32))
    cum = jnp.cumsum(dA, axis=2)
    last = cum[:, :, -1]
    cum_h = jnp.moveaxis(cum, 2, 3)
    seg = cum_h[..., :, None] - cum_h[..., None, :]
    mask = jnp.tril(jnp.ones((L, L), dtype=bool))
    decay = jnp.where(mask, jnp.exp(jnp.where(mask, seg, 0.0)), 0.0)
    scores = jnp.einsum('bnihs,bnjhs->bnhij', Ch, Bh) * decay * jnp.moveaxis(dt, 2, 3)[..., None, :]
    y = jnp.einsum('bnhij,bnjhp->bnihp', scores, xh)
    w_end = jnp.exp(last[:, :, None] - cum) * dt
    dS = jnp.einsum('bnjh,bnjhs,bnjhp->bnhps', w_end, Bh, xh)

    def step(S, inp):
        dec, ds = inp
        return dec[..., None, None] * S + ds, S

    S_last, S_prev = lax.scan(step, S0.astype(f32), (jnp.moveaxis(jnp.exp(last), 1, 0), jnp.moveaxis(dS, 1, 0)))
    S_prev = jnp.moveaxis(S_prev, 0, 1)
    y = y + jnp.einsum('bnihs,bnhps->bnihp', Ch, S_prev) * jnp.exp(cum)[..., None]
    y = y + d_skip.astype(f32)[:, None] * xh
    y = y.reshape(B, T, SSD_INNER) * jax.nn.silu(z.astype(f32))
    y = rmsnorm(y, norm_g)
    return y.astype(z.dtype), new_buf, S_last.astype(S0.dtype)


def conv_ffn(xn, buf, w_gate, w_up, conv_w, conv_b, w_down):
    g, new_buf = causal_dwconv(xn @ w_gate, buf, conv_w, conv_b)
    return (jax.nn.gelu(g, approximate=True) * (xn @ w_up)) @ w_down, new_buf


def run_trunk(x, p, lru_conv, lru_h, gla_S, ssd_conv, ssd_S, ffn_conv, weights):
    (norm_mix, w_in, lru_conv_w, lru_conv_b, lru_w_r, lru_b_r, lru_w_i, lru_b_i, lru_lambda,
     gla_w_lr, gla_b_lr, gla_norm, ssd_conv_w, ssd_conv_b, ssd_dt_bias, ssd_a_log, ssd_d, ssd_norm,
     w_out, norm_ffn, ffn_w_gate, ffn_w_up, ffn_conv_w, ffn_conv_b, ffn_w_down,
     norm_ple, ple_w_gate, ple_w_proj, norm_final) = weights
    offsets = np.cumsum(IN_WIDTHS)[:-1].tolist()
    n_lc, n_lh, n_gs, n_sc, n_ss, n_fc = [], [], [], [], [], []
    for i in range(DEPTH):
        xn = rmsnorm(x, norm_mix[i])
        (a_x, a_g, b_q, b_k, b_v, b_g, b_lr, c_z, c_xbc, c_dt) = jnp.split(xn @ w_in[i], offsets, axis=-1)
        ya, nbuf_a, nh_a = rglru_mixer(a_x, a_g, lru_conv[i], lru_h[i], lru_conv_w[i], lru_conv_b[i],
                                       lru_w_r[i], lru_b_r[i], lru_w_i[i], lru_b_i[i], lru_lambda[i])
        yb, nS_b = gla_mixer(b_q, b_k, b_v, b_g, b_lr, gla_S[i], gla_w_lr[i], gla_b_lr[i], gla_norm[i])
        yc, nbuf_c, nS_c = ssd_mixer(c_z, c_xbc, c_dt, ssd_conv[i], ssd_S[i], ssd_conv_w[i], ssd_conv_b[i],
                                     ssd_dt_bias[i], ssd_a_log[i], ssd_d[i], ssd_norm[i])
        x = x + jnp.concatenate([ya, yb, yc], axis=-1) @ w_out[i]
        f, nbuf_f = conv_ffn(rmsnorm(x, norm_ffn[i]), ffn_conv[i], ffn_w_gate[i], ffn_w_up[i],
                             ffn_conv_w[i], ffn_conv_b[i], ffn_w_down[i])
        x = x + f
        gate = jax.nn.sigmoid(rmsnorm(x, norm_ple[i]) @ ple_w_gate[i])
        x = x + gate * (p[i] @ ple_w_proj[i])
        n_lc.append(nbuf_a)
        n_lh.append(nh_a)
        n_gs.append(nS_b)
        n_sc.append(nbuf_c)
        n_ss.append(nS_c)
        n_fc.append(nbuf_f)
    y = rmsnorm(x, norm_final)
    return (y, jnp.stack(n_lc), jnp.stack(n_lh), jnp.stack(n_gs), jnp.stack(n_sc), jnp.stack(n_ss), jnp.stack(n_fc))


def setup_inputs(seed: int = 0) -> dict:
    key = jax.random.key(seed)
    keys = jax.random.split(key, 48)
    counter = [0]

    def nk():
        counter[0] += 1
        return keys[counter[0] - 1]

    def normal(shape, scale):
        return jax.random.normal(nk(), shape, jnp.float32) * scale

    def gain(shape):
        return 1.0 + normal(shape, 0.02)

    a_pow = jax.random.uniform(nk(), (DEPTH, D_LRU), jnp.float32, 0.9, 0.999)
    sig = a_pow ** (1.0 / LRU_C)
    dt0 = jnp.exp(jax.random.uniform(nk(), (DEPTH, SSD_HEADS), jnp.float32, math.log(1e-3), math.log(1e-1)))
    return {
        "x_prompt": normal((BATCH, SEQ, D_MODEL), 1.0),
        "x_sample": normal((DEC_BATCH, DEC_SEQ, D_MODEL), 1.0),
        "state_lru_conv": normal((DEPTH, DEC_BATCH, LRU_CONV - 1, D_LRU), 1.0),
        "state_lru_h": normal((DEPTH, DEC_BATCH, D_LRU), 0.5),
        "state_gla": normal((DEPTH, DEC_BATCH, GLA_HEADS, GLA_DK, GLA_DV), 0.1),
        "state_ssd_conv": normal((DEPTH, DEC_BATCH, SSD_CONV - 1, SSD_XBC), 1.0),
        "state_ssd": normal((DEPTH, DEC_BATCH, SSD_HEADS, SSD_HEAD_DIM, SSD_STATE), 0.1),
        "state_ffn_conv": normal((DEPTH, DEC_BATCH, FFN_CONV - 1, D_FF), 1.0),
        "p_prompt": normal((DEPTH, BATCH, SEQ, D_PLE), 1.0),
        "p_sample": normal((DEPTH, DEC_BATCH, DEC_SEQ, D_PLE), 1.0),
        "norm_mix": gain((DEPTH, D_MODEL)),
        "w_in": normal((DEPTH, D_MODEL, D_IN), D_MODEL ** -0.5),
        "lru_conv_w": normal((DEPTH, LRU_CONV, D_LRU), LRU_CONV ** -0.5),
        "lru_conv_b": normal((DEPTH, D_LRU), 0.01),
        "lru_w_r": normal((DEPTH, LRU_HEADS, LRU_HEAD_DIM, LRU_HEAD_DIM), LRU_HEAD_DIM ** -0.5),
        "lru_b_r": normal((DEPTH, D_LRU), 0.01),
        "lru_w_i": normal((DEPTH, LRU_HEADS, LRU_HEAD_DIM, LRU_HEAD_DIM), LRU_HEAD_DIM ** -0.5),
        "lru_b_i": normal((DEPTH, D_LRU), 0.01),
        "lru_lambda": jnp.log(sig) - jnp.log1p(-sig),
        "gla_w_lr": normal((DEPTH, GLA_RANK, GLA_KEY), GLA_RANK ** -0.5),
        "gla_b_lr": normal((DEPTH, GLA_KEY), 0.1),
        "gla_norm": gain((DEPTH, GLA_VAL)),
        "ssd_conv_w": normal((DEPTH, SSD_CONV, SSD_XBC), SSD_CONV ** -0.5),
        "ssd_conv_b": normal((DEPTH, SSD_XBC), 0.01),
        "ssd_dt_bias": dt0 + jnp.log(-jnp.expm1(-dt0)),
        "ssd_a_log": jnp.log(jax.random.uniform(nk(), (DEPTH, SSD_HEADS), jnp.float32, 1.0, 16.0)),
        "ssd_d": gain((DEPTH, SSD_HEADS)),
        "ssd_norm": gain((DEPTH, SSD_INNER)),
        "w_out": normal((DEPTH, D_MIX, D_MODEL), D_MIX ** -0.5),
        "norm_ffn": gain((DEPTH, D_MODEL)),
        "ffn_w_gate": normal((DEPTH, D_MODEL, D_FF), D_MODEL ** -0.5),
        "ffn_w_up": normal((DEPTH, D_MODEL, D_FF), D_MODEL ** -0.5),
        "ffn_conv_w": normal((DEPTH, FFN_CONV, D_FF), FFN_CONV ** -0.5),
        "ffn_conv_b": normal((DEPTH, D_FF), 0.01),
        "ffn_w_down": normal((DEPTH, D_FF, D_MODEL), D_FF ** -0.5),
        "norm_ple": gain((DEPTH, D_MODEL)),
        "ple_w_gate": normal((DEPTH, D_MODEL, D_MODEL), D_MODEL ** -0.5),
        "ple_w_proj": normal((DEPTH, D_PLE, D_MODEL), D_PLE ** -0.5),
        "norm_final": gain((D_MODEL,)),
    }


def reference(x_prompt, x_sample, state_lru_conv, state_lru_h, state_gla, state_ssd_conv, state_ssd, state_ffn_conv,
              p_prompt, p_sample, norm_mix, w_in, lru_conv_w, lru_conv_b, lru_w_r, lru_b_r, lru_w_i, lru_b_i,
              lru_lambda, gla_w_lr, gla_b_lr, gla_norm, ssd_conv_w, ssd_conv_b, ssd_dt_bias, ssd_a_log, ssd_d,
              ssd_norm, w_out, norm_ffn, ffn_w_gate, ffn_w_up, ffn_conv_w, ffn_conv_b, ffn_w_down, norm_ple,
              ple_w_gate, ple_w_proj, norm_final):
    weights = (norm_mix, w_in, lru_conv_w, lru_conv_b, lru_w_r, lru_b_r, lru_w_i, lru_b_i, lru_lambda,
               gla_w_lr, gla_b_lr, gla_norm, ssd_conv_w, ssd_conv_b, ssd_dt_bias, ssd_a_log, ssd_d, ssd_norm,
               w_out, norm_ffn, ffn_w_gate, ffn_w_up, ffn_conv_w, ffn_conv_b, ffn_w_down,
               norm_ple, ple_w_gate, ple_w_proj, norm_final)
    bp = x_prompt.shape[0]
    dt = x_prompt.dtype
    (y_prompt, p_lru_conv, p_lru_h, p_gla, p_ssd_conv, p_ssd, p_ffn_conv) = run_trunk(
        x_prompt, p_prompt,
        jnp.zeros((DEPTH, bp, LRU_CONV - 1, D_LRU), dt),
        jnp.zeros((DEPTH, bp, D_LRU), dt),
        jnp.zeros((DEPTH, bp, GLA_HEADS, GLA_DK, GLA_DV), dt),
        jnp.zeros((DEPTH, bp, SSD_CONV - 1, SSD_XBC), dt),
        jnp.zeros((DEPTH, bp, SSD_HEADS, SSD_HEAD_DIM, SSD_STATE), dt),
        jnp.zeros((DEPTH, bp, FFN_CONV - 1, D_FF), dt),
        weights)
    (y_sample, s_lru_conv, s_lru_h, s_gla, s_ssd_conv, s_ssd, s_ffn_conv) = run_trunk(
        x_sample, p_sample, state_lru_conv, state_lru_h, state_gla, state_ssd_conv, state_ssd, state_ffn_conv,
        weights)
    return (y_prompt, y_sample, p_lru_conv, p_lru_h, p_gla, p_ssd_conv, p_ssd, p_ffn_conv,
            s_lru_conv, s_lru_h, s_gla, s_ssd_conv, s_ssd, s_ffn_conv)
```

```python
import functools
import math

import numpy as np
import jax
import jax.numpy as jnp
from jax import lax
from jax.experimental import pallas as pl
from jax.experimental.pallas import tpu as pltpu

F32 = jnp.float32
BF16 = jnp.bfloat16

D_MODEL = 1024
DEPTH = 2
CHUNK = 64
EPS = 1e-6
D_PLE = 256
D_FF = 3072
FFN_CONV = 3
D_LRU = 512
LRU_HEADS = 8
LRU_CONV = 4
LRU_C = 8.0
GLA_HEADS = 4
GLA_DK = 64
GLA_DV = 128
GLA_KEY = GLA_HEADS * GLA_DK
GLA_VAL = GLA_HEADS * GLA_DV
GLA_RANK = 16
GLA_TAU = 16.0
SSD_HEADS = 8
SSD_HEAD_DIM = 64
SSD_INNER = SSD_HEADS * SSD_HEAD_DIM
SSD_GROUPS = 2
SSD_STATE = 128
SSD_CONV = 4
SSD_XBC = SSD_INNER + 2 * SSD_GROUPS * SSD_STATE
D_MIX = D_LRU + GLA_VAL + SSD_INNER

V7X_SUBLANES = 8
V7X_LANES = 128
V7X_VMEM_BYTES = 64 * 1024 * 1024

C_AX = 0
C_AG = C_AX + D_LRU
C_Q = C_AG + D_LRU
C_K = C_Q + GLA_KEY
C_V = C_K + GLA_KEY
C_G = C_V + GLA_VAL
C_Z = C_G + GLA_VAL
C_XBC = C_Z + SSD_INNER
C_AUX = C_XBC + SSD_XBC
D_IN_PAD = C_AUX + V7X_LANES
AUX_DT = GLA_RANK

ROWS = 512
PROJ_TILE = 512
FF_TILE = 512


def _mm(a, b):
    return jnp.dot(a, b, preferred_element_type=F32)


def _mm_nt(a, b):
    return lax.dot_general(a, b, (((1,), (1,)), ((), ())), preferred_element_type=F32)


def _split3(x):
    hi = x.astype(BF16)
    r1 = x - hi.astype(F32)
    mid = r1.astype(BF16)
    lo = (r1 - mid.astype(F32)).astype(BF16)
    return hi, mid, lo


def _mm_exact_lhs(c, x):
    hi, mid, lo = _split3(x)
    return (_mm(c, lo) + _mm(c, mid)) + _mm(c, hi)


def _mm_exact_rhs(x, c):
    hi, mid, lo = _split3(x)
    return (_mm(lo, c) + _mm(mid, c)) + _mm(hi, c)


def _rms(x, g):
    return x * lax.rsqrt(jnp.mean(x * x, axis=-1, keepdims=True) + EPS) * g


def _softplus(x):
    return jnp.maximum(x, 0.0) + jnp.log1p(jnp.exp(-jnp.abs(x)))


def _gelu_tanh(x):
    c = math.sqrt(2.0 / math.pi)
    return x * (0.5 * (1.0 + jnp.tanh(c * (x + 0.044715 * (x * x * x)))))


def _silu(x):
    return x * jax.nn.sigmoid(x)


def _neg_expm1(y):
    u = jnp.exp(y)
    d = 1.0 - u
    near = jnp.where(d == 0.0, -y, d * y / jnp.log(u))
    return jnp.where(jnp.abs(y) < 0.35, near, d)


def _causal_conv(x, prev, w_ref, b, width):
    p = jnp.concatenate([prev, x], axis=0)
    y = b + w_ref[width - 1:width, :] * x
    for s in range(1, width):
        y = y + w_ref[width - 1 - s:width - s, :] * pltpu.roll(p, s, 0)[V7X_SUBLANES:]
    return y


def _lane_iota(shape):
    return lax.broadcasted_iota(jnp.int32, shape, len(shape) - 1)


def _row_iota(shape):
    return lax.broadcasted_iota(jnp.int32, shape, len(shape) - 2)


def _lru_chunk(r0, s, proj_ref, ymix_ref, lc_ref, lh_ref, cw_ref, cb_ref, wr_ref, br_ref, wi_ref, bi_ref, c_lam):
    rows = pl.ds(r0, CHUNK)
    ax = proj_ref[rows, C_AX:C_AX + D_LRU]
    ag = proj_ref[rows, C_AG:C_AG + D_LRU]
    xc = _causal_conv(ax, lc_ref[s], cw_ref, cb_ref[...], LRU_CONV)
    lc_ref[s] = ax[CHUNK - V7X_SUBLANES:]
    xcb = xc.astype(BF16)
    half = D_LRU // 2
    r_pre = jnp.concatenate([_mm(xcb[:, :half], wr_ref[0]), _mm(xcb[:, half:], wr_ref[1])], axis=1) + br_ref[...]
    i_pre = jnp.concatenate([_mm(xcb[:, :half], wi_ref[0]), _mm(xcb[:, half:], wi_ref[1])], axis=1) + bi_ref[...]
    log_a = c_lam * jax.nn.sigmoid(r_pre)
    a = jnp.exp(log_a)
    u = jnp.sqrt(_neg_expm1(2.0 * log_a)) * (jax.nn.sigmoid(i_pre) * xc)
    row = _row_iota((V7X_SUBLANES, D_LRU))
    h_prev = lh_ref[s]
    hs = []
    for g in range(CHUNK // V7X_SUBLANES):
        ag_ = a[g * V7X_SUBLANES:(g + 1) * V7X_SUBLANES]
        ug_ = u[g * V7X_SUBLANES:(g + 1) * V7X_SUBLANES]
        for sh in (1, 2, 4):
            keep = row >= sh
            a_sh = jnp.where(keep, pltpu.roll(ag_, sh, 0), 1.0)
            u_sh = jnp.where(keep, pltpu.roll(ug_, sh, 0), 0.0)
            ug_ = ug_ + ag_ * u_sh
            ag_ = ag_ * a_sh
        hg = ug_ + ag_ * h_prev
        h_prev = hg[V7X_SUBLANES - 1:]
        hs.append(hg)
    lh_ref[s] = h_prev
    h = jnp.concatenate(hs, axis=0)
    ymix_ref[rows, 0:D_LRU] = (h * _gelu_tanh(ag)).astype(BF16)


def _gla_chunk(r0, s, proj_ref, ymix_ref, gs_ref, tri, wlr_ref, blr_ref, gn_ref):
    rows = pl.ds(r0, CHUNK)
    q = proj_ref[rows, C_Q:C_Q + GLA_KEY]
    k = proj_ref[rows, C_K:C_K + GLA_KEY]
    v = proj_ref[rows, C_V:C_V + GLA_VAL].astype(BF16)
    g = proj_ref[rows, C_G:C_G + GLA_VAL]
    aux = proj_ref[rows, C_AUX:C_AUX + V7X_LANES]
    la = -_softplus(-(_mm(aux.astype(BF16), wlr_ref[...]) + blr_ref[...])) * (1.0 / GLA_TAU)
    cum = _mm_exact_lhs(tri, la)
    last = cum[CHUNK - 1:]
    q_dec = q * (GLA_DK ** -0.5) * jnp.exp(cum)
    k_inv = (k * jnp.exp(-cum)).astype(BF16)
    k_end = k * jnp.exp(last - cum)
    lane = _lane_iota((CHUNK, GLA_KEY))
    qm = jnp.concatenate([jnp.where((lane >= h * GLA_DK) & (lane < (h + 1) * GLA_DK), q_dec, 0.0).astype(BF16)
                          for h in range(GLA_HEADS)], axis=0)
    sc = _mm_nt(qm, k_inv)
    shp = (GLA_HEADS * CHUNK, CHUNK)
    causal = (_row_iota(shp) & (CHUNK - 1)) >= _lane_iota(shp)
    sc = jnp.where(causal, sc, 0.0).astype(BF16)
    state = gs_ref[s]
    o_inter = _mm(qm, state.astype(BF16))
    k_end_t = k_end.T.astype(BF16)
    outs, ds = [], []
    for h in range(GLA_HEADS):
        hr = slice(h * CHUNK, (h + 1) * CHUNK)
        vh = v[:, h * GLA_DV:(h + 1) * GLA_DV]
        o = _mm(sc[hr], vh) + o_inter[hr]
        outs.append(o * lax.rsqrt(jnp.mean(o * o, axis=-1, keepdims=True) + EPS))
        ds.append(_mm(k_end_t[h * GLA_DK:(h + 1) * GLA_DK], vh))
    dec_col = jnp.broadcast_to(jnp.exp(last), (GLA_DV, GLA_KEY)).T
    gs_ref[s] = dec_col * state + jnp.concatenate(ds, axis=0)
    o = jnp.concatenate(outs, axis=1) * gn_ref[...]
    ymix_ref[rows, D_LRU:D_LRU + GLA_VAL] = (o * _silu(g)).astype(BF16)


def _ssd_chunk(r0, s, proj_ref, ymix_ref, sc_ref, ss_ref, tri, xp_ref, xp2_ref, cw_ref, cb_ref, dtb_ref, a_aux,
               drep_ref, sn_ref):
    rows = pl.ds(r0, CHUNK)
    z = proj_ref[rows, C_Z:C_Z + SSD_INNER]
    xbc = proj_ref[rows, C_XBC:C_XBC + SSD_XBC]
    aux = proj_ref[rows, C_AUX:C_AUX + V7X_LANES]
    xc = _silu(_causal_conv(xbc, sc_ref[s], cw_ref, cb_ref[...], SSD_CONV))
    sc_ref[s] = xbc[CHUNK - V7X_SUBLANES:]
    xs = xc[:, :SSD_INNER]
    gw = SSD_STATE
    lane = _lane_iota((CHUNK, V7X_LANES))
    is_dt = (lane >= AUX_DT) & (lane < AUX_DT + SSD_HEADS)
    dt_aux = jnp.where(is_dt, _softplus(aux + dtb_ref[...]), 0.0)
    cum_aux = _mm_exact_lhs(tri, dt_aux * a_aux)
    e = _mm_exact_rhs(jnp.concatenate([cum_aux, dt_aux], axis=0), xp_ref[...])
    e_cum, e_dt = e[:CHUNK], e[CHUNK:]
    last_e = e_cum[CHUNK - 1:]
    xw = xs * (jnp.exp(last_e - e_cum) * e_dt)
    last_rows = _mm_exact_rhs(cum_aux[CHUNK - V7X_SUBLANES:], xp2_ref[...])
    dec_rows = jnp.exp(last_rows[V7X_SUBLANES - 1:])
    shp = (CHUNK, V7X_LANES)
    col_t = _lane_iota(shp) & (CHUNK - 1)
    diag = _row_iota(shp) == col_t
    causal = _row_iota(shp) >= col_t
    low = _lane_iota(shp) < SSD_HEAD_DIM
    heads_per_group = SSD_HEADS // SSD_GROUPS
    ys = []
    for grp in range(SSD_GROUPS):
        bg = xc[:, SSD_INNER + grp * gw:SSD_INNER + (grp + 1) * gw].astype(BF16)
        cg = xc[:, SSD_INNER + SSD_GROUPS * gw + grp * gw:SSD_INNER + SSD_GROUPS * gw + (grp + 1) * gw].astype(BF16)
        sc2 = _mm_nt(cg, jnp.concatenate([bg, bg], axis=0))
        hrows = slice(grp * heads_per_group * SSD_HEAD_DIM, (grp + 1) * heads_per_group * SSD_HEAD_DIM)
        state = ss_ref[s, hrows, :]
        glanes = slice(grp * heads_per_group * SSD_HEAD_DIM, (grp + 1) * heads_per_group * SSD_HEAD_DIM)
        y_inter = _mm_nt(cg, state.astype(BF16)) * jnp.exp(e_cum[:, glanes])
        for pair in range(heads_per_group // 2):
            lo = (grp * heads_per_group + 2 * pair) * SSD_HEAD_DIM
            pl_ = slice(lo, lo + V7X_LANES)
            ec = e_cum[:, pl_]
            rc = jnp.sum(jnp.where(diag, ec, 0.0), axis=0, keepdims=True)
            rd = jnp.sum(jnp.where(diag, e_dt[:, pl_], 0.0), axis=0, keepdims=True)
            decay = jnp.where(causal, jnp.exp(jnp.where(causal, ec - rc, 0.0)), 0.0)
            m = (sc2 * decay * rd).astype(BF16)
            xq = xs[:, pl_]
            rhs = jnp.concatenate([jnp.where(low, xq, 0.0), jnp.where(low, 0.0, xq)], axis=0).astype(BF16)
            ys.append(_mm(m, rhs) + y_inter[:, 2 * pair * SSD_HEAD_DIM:(2 * pair + 2) * SSD_HEAD_DIM])
        ds = _mm(xw[:, glanes].T.astype(BF16), bg)
        for hh in range(heads_per_group):
            h = grp * heads_per_group + hh
            rr = slice(hh * SSD_HEAD_DIM, (hh + 1) * SSD_HEAD_DIM)
            dec = dec_rows[:, h * V7X_LANES:(h + 1) * V7X_LANES]
            ss_ref[s, h * SSD_HEAD_DIM:(h + 1) * SSD_HEAD_DIM, :] = dec * state[rr] + ds[rr]
    y = jnp.concatenate(ys, axis=1) + drep_ref[...] * xs
    y = _rms(y * _silu(z), sn_ref[...])
    ymix_ref[rows, D_LRU + GLA_VAL:D_MIX] = y.astype(BF16)


def _mixer_kernel(x_ref, lc_in, lh_in, gs_in, sc_in, ss_in,
                  nm_ref, win_ref, lcw_ref, lcb_ref, wr_ref, br_ref, wi_ref, bi_ref, lam_ref,
                  wlr_ref, blr_ref, gn_ref, scw_ref, scb_ref, dtb_ref, alog_ref, drep_ref, sn_ref, wout_ref,
                  tri_ref, xp_ref, xp2_ref,
                  o_ref, lc_ref, lh_ref, gs_ref, sc_ref, ss_ref,
                  xn_ref, proj_ref, ymix_ref, *, n_seq, n_chunk):
    @pl.when(pl.program_id(1) == 0)
    def _():
        lc_ref[...] = lc_in[...]
        lh_ref[...] = lh_in[...]
        gs_ref[...] = gs_in[...]
        sc_ref[...] = sc_in[...]
        ss_ref[...] = ss_in[...]

    xn_ref[...] = _rms(x_ref[...], nm_ref[...]).astype(BF16)
    for lo in range(0, D_IN_PAD, PROJ_TILE):
        hi = min(lo + PROJ_TILE, D_IN_PAD)
        proj_ref[:, lo:hi] = _mm(xn_ref[...], win_ref[:, lo:hi])

    c_lam = -LRU_C * _softplus(-lam_ref[...])
    lane = _lane_iota((1, V7X_LANES))
    a_aux = jnp.where((lane >= AUX_DT) & (lane < AUX_DT + SSD_HEADS), -jnp.exp(alog_ref[...]), 0.0)
    tri = tri_ref[...]

    def chunk(c, carry):
        r0 = pl.multiple_of(c * CHUNK, CHUNK)
        s = 0 if n_seq == 1 else (c if n_chunk == 1 else c // n_chunk)
        _lru_chunk(r0, s, proj_ref, ymix_ref, lc_ref, lh_ref, lcw_ref, lcb_ref, wr_ref, br_ref, wi_ref, bi_ref, c_lam)
        _gla_chunk(r0, s, proj_ref, ymix_ref, gs_ref, tri, wlr_ref, blr_ref, gn_ref)
        _ssd_chunk(r0, s, proj_ref, ymix_ref, sc_ref, ss_ref, tri, xp_ref, xp2_ref, scw_ref, scb_ref, dtb_ref, a_aux,
                   drep_ref, sn_ref)
        return carry

    lax.fori_loop(0, n_seq * n_chunk, chunk, 0)
    o_ref[...] = x_ref[...] + _mm(ymix_ref[...], wout_ref[...])


def _ffn_kernel(x_ref, p_ref, fc_in, nf_ref, wg_ref, wu_ref, cw_ref, cb_ref, wd_ref, npl_ref, pwg_ref, pwp_ref,
                nfin_ref, o_ref, fc_ref, xn_ref, h_ref, *, n_seq, final):
    @pl.when(pl.program_id(1) == 0)
    def _():
        fc_ref[...] = fc_in[...]

    seq_rows = ROWS // n_seq
    xn_ref[...] = _rms(x_ref[...], nf_ref[...]).astype(BF16)
    for lo in range(0, D_FF, FF_TILE):
        cols = slice(lo, lo + FF_TILE)
        g = _mm(xn_ref[...], wg_ref[:, cols])
        u = _mm(xn_ref[...], wu_ref[:, cols])
        pieces = []
        for s in range(n_seq):
            gs = g[s * seq_rows:(s + 1) * seq_rows]
            p = jnp.concatenate([fc_ref[s, :, cols], gs], axis=0)
            c = cb_ref[:, cols] + cw_ref[2:3, cols] * gs
            c = c + cw_ref[1:2, cols] * pltpu.roll(p, 1, 0)[V7X_SUBLANES:]
            c = c + cw_ref[0:1, cols] * pltpu.roll(p, 2, 0)[V7X_SUBLANES:]
            fc_ref[s, :, cols] = gs[seq_rows - V7X_SUBLANES:]
            pieces.append(c)
        gc = pieces[0] if n_seq == 1 else jnp.concatenate(pieces, axis=0)
        h_ref[:, cols] = (_gelu_tanh(gc) * u).astype(BF16)
    o_ref[...] = x_ref[...] + _mm(h_ref[...], wd_ref[...])
    gate = jax.nn.sigmoid(_mm(_rms(o_ref[...], npl_ref[...]).astype(BF16), pwg_ref[...]))
    y = o_ref[...] + gate * _mm(p_ref[...].astype(BF16), pwp_ref[...])
    if final:
        y = _rms(y, nfin_ref[...])
    o_ref[...] = y


def _vmem_limit(n_bytes):
    return int(min(n_bytes + (8 << 20), V7X_VMEM_BYTES - (4 << 20)))


def _nbytes(shape, dtype):
    return int(np.prod(shape)) * jnp.dtype(dtype).itemsize


def _weight_spec(arr, layer):
    return pl.BlockSpec((None,) + arr.shape[1:], lambda o, t: (layer, 0, 0), pipeline_mode=pl.Buffered(1))


def _const_spec(arr):
    return pl.BlockSpec(arr.shape, lambda o, t: (0, 0), pipeline_mode=pl.Buffered(1))


def _mixer_call(x, states, w, consts, layer, n_seq, n_chunk):
    n_rows = x.shape[0]
    n_batch = states[0].shape[0]
    n_outer = n_batch // n_seq
    n_t = n_rows // (ROWS * n_outer)
    assert n_seq * n_chunk * CHUNK == ROWS and n_outer * n_t * ROWS == n_rows
    row_spec = pl.BlockSpec((ROWS, D_MODEL), lambda o, t: (o * n_t + t, 0))
    state_specs = [pl.BlockSpec((n_seq,) + st.shape[1:], lambda o, t: (o, 0, 0)) for st in states]
    wnames = ("norm_mix", "w_in", "lru_conv_w", "lru_conv_b", "lru_wr", "lru_b_r", "lru_wi", "lru_b_i", "lru_lambda",
              "gla_w_lr", "gla_b_lr", "gla_norm", "ssd_conv_w", "ssd_conv_b", "ssd_dtb", "ssd_alog", "ssd_drep",
              "ssd_norm", "w_out")
    ws = [w[n] for n in wnames]
    w_specs = []
    for a in ws:
        if a.ndim == 4:
            w_specs.append(pl.BlockSpec((None,) + a.shape[1:], lambda o, t: (layer, 0, 0, 0),
                                        pipeline_mode=pl.Buffered(1)))
        else:
            w_specs.append(_weight_spec(a, layer))
    c_specs = [_const_spec(c) for c in consts]
    scratch = [pltpu.VMEM((ROWS, D_MODEL), BF16), pltpu.VMEM((ROWS, D_IN_PAD), F32), pltpu.VMEM((ROWS, D_MIX), BF16)]
    est = (4 * _nbytes((ROWS, D_MODEL), F32)
           + 4 * sum(_nbytes((n_seq,) + st.shape[1:], F32) for st in states)
           + sum(_nbytes(a.shape[1:], a.dtype) for a in ws) + sum(_nbytes(c.shape, c.dtype) for c in consts)
           + _nbytes((ROWS, D_MODEL), BF16) + _nbytes((ROWS, D_IN_PAD), F32) + _nbytes((ROWS, D_MIX), BF16))
    out = pl.pallas_call(
        functools.partial(_mixer_kernel, n_seq=n_seq, n_chunk=n_chunk),
        grid=(n_outer, n_t),
        in_specs=[row_spec] + state_specs + w_specs + c_specs,
        out_specs=[row_spec] + state_specs,
        out_shape=[jax.ShapeDtypeStruct(x.shape, F32)] + [jax.ShapeDtypeStruct(st.shape, F32) for st in states],
        scratch_shapes=scratch,
        compiler_params=pltpu.CompilerParams(dimension_semantics=("arbitrary", "arbitrary"),
                                             vmem_limit_bytes=_vmem_limit(est)),
        name=f"mixer_l{layer}_s{n_seq}",
    )(x, *states, *ws, *consts)
    return out[0], out[1:]


def _ffn_call(x, p, fc, w, layer, n_seq, final):
    n_rows = x.shape[0]
    n_outer = fc.shape[0] // n_seq
    n_t = n_rows // (ROWS * n_outer)
    assert n_outer * n_t * ROWS == n_rows
    row_spec = pl.BlockSpec((ROWS, D_MODEL), lambda o, t: (o * n_t + t, 0))
    p_spec = pl.BlockSpec((None, ROWS, D_PLE), lambda o, t: (layer, o * n_t + t, 0))
    fc_spec = pl.BlockSpec((n_seq,) + fc.shape[1:], lambda o, t: (o, 0, 0))
    wnames = ("norm_ffn", "ffn_w_gate", "ffn_w_up", "ffn_conv_w", "ffn_conv_b", "ffn_w_down", "norm_ple",
              "ple_w_gate", "ple_w_proj")
    ws = [w[n] for n in wnames]
    nfin = w["norm_final"]
    scratch = [pltpu.VMEM((ROWS, D_MODEL), BF16), pltpu.VMEM((ROWS, D_FF), BF16)]
    est = (4 * _nbytes((ROWS, D_MODEL), F32) + 2 * _nbytes((ROWS, D_PLE), F32)
           + 4 * _nbytes((n_seq,) + fc.shape[1:], F32)
           + sum(_nbytes(a.shape[1:], a.dtype) for a in ws)
           + _nbytes((ROWS, D_MODEL), BF16) + _nbytes((ROWS, D_FF), BF16)
           + 4 * _nbytes((ROWS, FF_TILE), F32))
    out = pl.pallas_call(
        functools.partial(_ffn_kernel, n_seq=n_seq, final=final),
        grid=(n_outer, n_t),
        in_specs=[row_spec, p_spec, fc_spec] + [_weight_spec(a, layer) for a in ws] + [_const_spec(nfin)],
        out_specs=[row_spec, fc_spec],
        out_shape=[jax.ShapeDtypeStruct(x.shape, F32), jax.ShapeDtypeStruct(fc.shape, F32)],
        scratch_shapes=scratch,
        compiler_params=pltpu.CompilerParams(dimension_semantics=("arbitrary", "arbitrary"),
                                             vmem_limit_bytes=_vmem_limit(est)),
        name=f"ffn_l{layer}_s{n_seq}",
    )(x, p, fc, *ws, nfin)
    return out[0], out[1]


def _block_diag_gate(w):
    d = w.shape[0]
    w5 = w.reshape(d, 2, 4, 64, 64)
    out = jnp.einsum("dghij,hk->dghikj", w5, jnp.eye(4, dtype=w.dtype))
    return out.reshape(d, 2, 256, 256).astype(BF16)


def _prepare_weights(norm_mix, w_in, lru_conv_w, lru_conv_b, lru_w_r, lru_b_r, lru_w_i, lru_b_i, lru_lambda,
                     gla_w_lr, gla_b_lr, gla_norm, ssd_conv_w, ssd_conv_b, ssd_dt_bias, ssd_a_log, ssd_d, ssd_norm,
                     w_out, norm_ffn, ffn_w_gate, ffn_w_up, ffn_conv_w, ffn_conv_b, ffn_w_down, norm_ple,
                     ple_w_gate, ple_w_proj, norm_final):
    def row(v):
        return v.reshape(DEPTH, 1, -1)

    lr0 = 2 * D_LRU + 2 * GLA_KEY + 2 * GLA_VAL
    dt0 = lr0 + GLA_RANK + SSD_INNER + SSD_XBC
    w_in_r = jnp.concatenate(
        [w_in[:, :, :lr0], w_in[:, :, lr0 + GLA_RANK:dt0], w_in[:, :, lr0:lr0 + GLA_RANK], w_in[:, :, dt0:],
         jnp.zeros((DEPTH, D_MODEL, V7X_LANES - GLA_RANK - SSD_HEADS), w_in.dtype)], axis=2).astype(BF16)
    pad_aux = lambda v: jnp.pad(v, ((0, 0), (AUX_DT, V7X_LANES - AUX_DT - SSD_HEADS)))
    return {
        "norm_mix": row(norm_mix), "w_in": w_in_r,
        "lru_conv_w": lru_conv_w, "lru_conv_b": row(lru_conv_b),
        "lru_wr": _block_diag_gate(lru_w_r), "lru_b_r": row(lru_b_r),
        "lru_wi": _block_diag_gate(lru_w_i), "lru_b_i": row(lru_b_i), "lru_lambda": row(lru_lambda),
        "gla_w_lr": jnp.pad(gla_w_lr, ((0, 0), (0, V7X_LANES - GLA_RANK), (0, 0))).astype(BF16),
        "gla_b_lr": row(gla_b_lr), "gla_norm": row(gla_norm),
        "ssd_conv_w": ssd_conv_w, "ssd_conv_b": row(ssd_conv_b),
        "ssd_dtb": row(pad_aux(ssd_dt_bias)), "ssd_alog": row(pad_aux(ssd_a_log)),
        "ssd_drep": row(jnp.repeat(ssd_d, SSD_HEAD_DIM, axis=1)), "ssd_norm": row(ssd_norm),
        "w_out": w_out.astype(BF16),
        "norm_ffn": row(norm_ffn), "ffn_w_gate": ffn_w_gate.astype(BF16), "ffn_w_up": ffn_w_up.astype(BF16),
        "ffn_conv_w": ffn_conv_w, "ffn_conv_b": row(ffn_conv_b), "ffn_w_down": ffn_w_down.astype(BF16),
        "norm_ple": row(norm_ple), "ple_w_gate": ple_w_gate.astype(BF16), "ple_w_proj": ple_w_proj.astype(BF16),
        "norm_final": norm_final.reshape(1, -1),
    }


def _constants():
    i = np.arange(CHUNK)
    tri = (i[None, :] <= i[:, None]).astype(np.float32)
    expand = np.zeros((V7X_LANES, SSD_INNER), np.float32)
    expand2 = np.zeros((V7X_LANES, SSD_HEADS * V7X_LANES), np.float32)
    for h in range(SSD_HEADS):
        expand[AUX_DT + h, h * SSD_HEAD_DIM:(h + 1) * SSD_HEAD_DIM] = 1.0
        expand2[AUX_DT + h, h * V7X_LANES:(h + 1) * V7X_LANES] = 1.0
    return [jnp.asarray(tri, BF16), jnp.asarray(expand, BF16), jnp.asarray(expand2, BF16)]


def _pad_rows_front(buf):
    return jnp.pad(buf, ((0, 0), (0, 0), (V7X_SUBLANES - buf.shape[2], 0), (0, 0)))


def _run_trunk(x, p, lru_conv, lru_h, gla_s, ssd_conv, ssd_s, ffn_conv, w, consts, n_seq, n_chunk):
    b, t, _ = x.shape
    xr = x.reshape(b * t, D_MODEL)
    pr = p.reshape(DEPTH, b * t, D_PLE)
    lc = _pad_rows_front(lru_conv)
    lh = lru_h.reshape(DEPTH, b, 1, D_LRU)
    gs = gla_s.reshape(DEPTH, b, GLA_KEY, GLA_DV)
    sc = _pad_rows_front(ssd_conv)
    ss = ssd_s.reshape(DEPTH, b, SSD_INNER, SSD_STATE)
    fc = _pad_rows_front(ffn_conv)
    new = []
    for layer in range(DEPTH):
        xr, st = _mixer_call(xr, [lc[layer], lh[layer], gs[layer], sc[layer], ss[layer]], w, consts, layer,
                             n_seq, n_chunk)
        xr, fcn = _ffn_call(xr, pr, fc[layer], w, layer, n_seq, layer == DEPTH - 1)
        new.append(list(st) + [fcn])
    stack = lambda k: jnp.stack([n[k] for n in new])
    return (xr.reshape(b, t, D_MODEL),
            stack(0)[:, :, V7X_SUBLANES - (LRU_CONV - 1):],
            stack(1).reshape(DEPTH, b, D_LRU),
            stack(2).reshape(DEPTH, b, GLA_HEADS, GLA_DK, GLA_DV),
            stack(3)[:, :, V7X_SUBLANES - (SSD_CONV - 1):],
            stack(4).reshape(DEPTH, b, SSD_HEADS, SSD_HEAD_DIM, SSD_STATE),
            stack(5)[:, :, V7X_SUBLANES - (FFN_CONV - 1):])


def kernel(x_prompt, x_sample, state_lru_conv, state_lru_h, state_gla, state_ssd_conv, state_ssd, state_ffn_conv, p_prompt, p_sample, norm_mix, w_in, lru_conv_w, lru_conv_b, lru_w_r, lru_b_r, lru_w_i, lru_b_i, lru_lambda, gla_w_lr, gla_b_lr, gla_norm, ssd_conv_w, ssd_conv_b, ssd_dt_bias, ssd_a_log, ssd_d, ssd_norm, w_out, norm_ffn, ffn_w_gate, ffn_w_up, ffn_conv_w, ffn_conv_b, ffn_w_down, norm_ple, ple_w_gate, ple_w_proj, norm_final):
    w = _prepare_weights(norm_mix, w_in, lru_conv_w, lru_conv_b, lru_w_r, lru_b_r, lru_w_i, lru_b_i, lru_lambda,
                         gla_w_lr, gla_b_lr, gla_norm, ssd_conv_w, ssd_conv_b, ssd_dt_bias, ssd_a_log, ssd_d,
                         ssd_norm, w_out, norm_ffn, ffn_w_gate, ffn_w_up, ffn_conv_w, ffn_conv_b, ffn_w_down,
                         norm_ple, ple_w_gate, ple_w_proj, norm_final)
    consts = _constants()
    bp = x_prompt.shape[0]
    dt = x_prompt.dtype
    zeros = lambda *shape: jnp.zeros((DEPTH, bp) + shape, dt)
    prompt = _run_trunk(x_prompt, p_prompt,
                        zeros(LRU_CONV - 1, D_LRU), zeros(D_LRU), zeros(GLA_HEADS, GLA_DK, GLA_DV),
                        zeros(SSD_CONV - 1, SSD_XBC), zeros(SSD_HEADS, SSD_HEAD_DIM, SSD_STATE),
                        zeros(FFN_CONV - 1, D_FF), w, consts, n_seq=1, n_chunk=ROWS // CHUNK)
    bs, ts, _ = x_sample.shape
    sample = _run_trunk(x_sample, p_sample, state_lru_conv, state_lru_h, state_gla, state_ssd_conv, state_ssd,
                        state_ffn_conv, w, consts, n_seq=ROWS // ts, n_chunk=ts // CHUNK)
    return (prompt[0], sample[0]) + tuple(prompt[1:]) + tuple(sample[1:])
```

```python
import functools
import math

import numpy as np
import jax
import jax.numpy as jnp
from jax import lax
from jax.experimental import pallas as pl
from jax.experimental.pallas import tpu as pltpu

F32 = jnp.float32
BF16 = jnp.bfloat16

D_MODEL = 1024
DEPTH = 2
CHUNK = 64
EPS = 1e-6
D_PLE = 256
D_FF = 3072
FFN_CONV = 3
D_LRU = 512
LRU_HEADS = 8
LRU_CONV = 4
LRU_C = 8.0
GLA_HEADS = 4
GLA_DK = 64
GLA_DV = 128
GLA_KEY = GLA_HEADS * GLA_DK
GLA_VAL = GLA_HEADS * GLA_DV
GLA_RANK = 16
GLA_TAU = 16.0
SSD_HEADS = 8
SSD_HEAD_DIM = 64
SSD_INNER = SSD_HEADS * SSD_HEAD_DIM
SSD_GROUPS = 2
SSD_STATE = 128
SSD_CONV = 4
SSD_XBC = SSD_INNER + 2 * SSD_GROUPS * SSD_STATE
D_MIX = D_LRU + GLA_VAL + SSD_INNER

V7X_SUBLANES = 8
V7X_LANES = 128
V7X_VMEM_BYTES = 64 * 1024 * 1024

C_AX = 0
C_AG = C_AX + D_LRU
C_Q = C_AG + D_LRU
C_K = C_Q + GLA_KEY
C_V = C_K + GLA_KEY
C_G = C_V + GLA_VAL
C_Z = C_G + GLA_VAL
C_XBC = C_Z + SSD_INNER
C_AUX = C_XBC + SSD_XBC
D_IN_PAD = C_AUX + V7X_LANES
AUX_DT = GLA_RANK

ROWS = 512
PROJ_TILE = 512
FF_TILE = 512


def _mm(a, b):
    return jnp.dot(a, b, preferred_element_type=F32)


def _mm_nt(a, b):
    return lax.dot_general(a, b, (((1,), (1,)), ((), ())), preferred_element_type=F32)


def _split3(x):
    hi = x.astype(BF16)
    r1 = x - hi.astype(F32)
    mid = r1.astype(BF16)
    lo = (r1 - mid.astype(F32)).astype(BF16)
    return hi, mid, lo


def _mm_exact_lhs(c, x):
    hi, mid, lo = _split3(x)
    return (_mm(c, lo) + _mm(c, mid)) + _mm(c, hi)


def _mm_exact_rhs(x, c):
    hi, mid, lo = _split3(x)
    return (_mm(lo, c) + _mm(mid, c)) + _mm(hi, c)


def _rms(x, g):
    return x * lax.rsqrt(jnp.mean(x * x, axis=-1, keepdims=True) + EPS) * g


def _softplus(x):
    return jnp.maximum(x, 0.0) + jnp.log1p(jnp.exp(-jnp.abs(x)))


def _gelu_tanh(x):
    c = math.sqrt(2.0 / math.pi)
    return x * (0.5 * (1.0 + jnp.tanh(c * (x + 0.044715 * (x * x * x)))))


def _silu(x):
    return x * jax.nn.sigmoid(x)


def _every_8th_row(ref, tile, start):
    return ref[tile, pl.ds(start, V7X_SUBLANES, stride=V7X_SUBLANES), :]


def _causal_conv_interleaved(stage_ref, x, prev, w_ref, b_ref, width):
    n_tiles = x.shape[1] // V7X_LANES
    for j in range(n_tiles):
        cols = slice(j * V7X_LANES, (j + 1) * V7X_LANES)
        stage_ref[j, 0:V7X_SUBLANES, :] = prev[:, cols]
        stage_ref[j, V7X_SUBLANES:V7X_SUBLANES + CHUNK, :] = x[:, cols]
    out = []
    for k in range(V7X_SUBLANES):
        tiles = []
        for j in range(n_tiles):
            cols = slice(j * V7X_LANES, (j + 1) * V7X_LANES)
            y = b_ref[:, cols] + w_ref[width - 1:width, cols] * _every_8th_row(stage_ref, j, V7X_SUBLANES + k)
            for s in range(1, width):
                y = y + w_ref[width - 1 - s:width - s, cols] * _every_8th_row(stage_ref, j, V7X_SUBLANES + k - s)
            tiles.append(y)
        out.append(jnp.concatenate(tiles, axis=1))
    return out


def _deinterleave(ref, parts):
    n_tiles = parts[0].shape[1] // V7X_LANES
    for k, part in enumerate(parts):
        for j in range(n_tiles):
            ref[j, pl.ds(k, V7X_SUBLANES, stride=V7X_SUBLANES), :] = part[:, j * V7X_LANES:(j + 1) * V7X_LANES]
    return jnp.concatenate([ref[j, 0:CHUNK, :] for j in range(n_tiles)], axis=1)


def _lane_iota(shape):
    return lax.broadcasted_iota(jnp.int32, shape, len(shape) - 1)


def _row_iota(shape):
    return lax.broadcasted_iota(jnp.int32, shape, len(shape) - 2)


def _trace_round_robin(*stages):
    live = list(stages)
    while live:
        for g in list(live):
            try:
                next(g)
            except StopIteration:
                live.remove(g)


def _lru_chunk(r0, s, proj_ref, ymix_ref, lc_ref, lh_ref, stage_ref, perm_ref, cw_ref, cb_ref, wr_ref, br_ref,
               wi_ref, bi_ref, c_lam):
    rows = pl.ds(r0, CHUNK)
    ax = proj_ref[rows, C_AX:C_AX + D_LRU]
    ag = proj_ref[rows, C_AG:C_AG + D_LRU]
    n_grp = CHUNK // V7X_SUBLANES
    xc = jnp.concatenate(_causal_conv_interleaved(stage_ref, ax, lc_ref[s], cw_ref, cb_ref, LRU_CONV), axis=0)
    lc_ref[s] = ax[CHUNK - V7X_SUBLANES:]
    xcb = xc.astype(BF16)
    half = D_LRU // 2
    r_pre = jnp.concatenate([_mm(xcb[:, :half], wr_ref[0]), _mm(xcb[:, half:], wr_ref[1])], axis=1) + br_ref[...]
    i_pre = jnp.concatenate([_mm(xcb[:, :half], wi_ref[0]), _mm(xcb[:, half:], wi_ref[1])], axis=1) + bi_ref[...]
    yield
    log_a = c_lam * jax.nn.sigmoid(r_pre)
    a = jnp.exp(log_a)
    u = jnp.sqrt(-jnp.tanh(log_a) * (1.0 + a * a)) * (jax.nn.sigmoid(i_pre) * xc)
    grp = lambda v, k: v[k * V7X_SUBLANES:(k + 1) * V7X_SUBLANES]
    a_run, u_run = [grp(a, 0)], [grp(u, 0)]
    for k in range(1, n_grp):
        u_run.append(grp(a, k) * u_run[-1] + grp(u, k))
        a_run.append(grp(a, k) * a_run[-1])
    row = _row_iota((V7X_SUBLANES, D_LRU))
    pa, pu = a_run[-1], u_run[-1]
    for sh in (1, 2, 4):
        keep = row >= sh
        a_sh = jnp.where(keep, pltpu.roll(pa, sh, 0), 1.0)
        u_sh = jnp.where(keep, pltpu.roll(pu, sh, 0), 0.0)
        pu = pu + pa * u_sh
        pa = pa * a_sh
    h_prev = lh_ref[s]
    h_end = pu + pa * h_prev
    h_in = jnp.where(row == 0, h_prev, pltpu.roll(h_end, 1, 0))
    lh_ref[s] = h_end[V7X_SUBLANES - 1:]
    h = _deinterleave(perm_ref, [u_run[k] + a_run[k] * h_in for k in range(n_grp)])
    ymix_ref[rows, 0:D_LRU] = (h * _gelu_tanh(ag)).astype(BF16)


def _gla_chunk(r0, s, proj_ref, ymix_ref, gs_ref, tri, wlr_ref, blr_ref, gn_ref):
    rows = pl.ds(r0, CHUNK)
    q = proj_ref[rows, C_Q:C_Q + GLA_KEY]
    k = proj_ref[rows, C_K:C_K + GLA_KEY]
    v = proj_ref[rows, C_V:C_V + GLA_VAL].astype(BF16)
    g = proj_ref[rows, C_G:C_G + GLA_VAL]
    aux = proj_ref[rows, C_AUX:C_AUX + V7X_LANES]
    la_pre = _mm(aux.astype(BF16), wlr_ref[...])
    yield
    la = -_softplus(-(la_pre + blr_ref[...])) * (1.0 / GLA_TAU)
    cum = _mm_exact_lhs(tri, la)
    yield
    last = cum[CHUNK - 1:]
    q_dec = q * (GLA_DK ** -0.5) * jnp.exp(cum)
    k_inv = (k * jnp.exp(-cum)).astype(BF16)
    k_end = k * jnp.exp(last - cum)
    lane = _lane_iota((CHUNK, GLA_KEY))
    qm = jnp.concatenate([jnp.where((lane >= h * GLA_DK) & (lane < (h + 1) * GLA_DK), q_dec, 0.0).astype(BF16)
                          for h in range(GLA_HEADS)], axis=0)
    sc = _mm_nt(qm, k_inv)
    state = gs_ref[s]
    o_inter = _mm(qm, state.astype(BF16))
    k_end_t = k_end.T.astype(BF16)
    dec_col = jnp.broadcast_to(jnp.exp(last), (GLA_DV, GLA_KEY)).T
    yield
    shp = (GLA_HEADS * CHUNK, CHUNK)
    causal = (_row_iota(shp) & (CHUNK - 1)) >= _lane_iota(shp)
    sc = jnp.where(causal, sc, 0.0).astype(BF16)
    outs, ds = [], []
    for h in range(GLA_HEADS):
        hr = slice(h * CHUNK, (h + 1) * CHUNK)
        vh = v[:, h * GLA_DV:(h + 1) * GLA_DV]
        outs.append(_mm(sc[hr], vh) + o_inter[hr])
        ds.append(_mm(k_end_t[h * GLA_DK:(h + 1) * GLA_DK], vh))
    yield
    gs_ref[s] = dec_col * state + jnp.concatenate(ds, axis=0)
    outs = [o * lax.rsqrt(jnp.mean(o * o, axis=-1, keepdims=True) + EPS) for o in outs]
    o = jnp.concatenate(outs, axis=1) * gn_ref[...]
    ymix_ref[rows, D_LRU:D_LRU + GLA_VAL] = (o * _silu(g)).astype(BF16)


def _ssd_chunk(r0, s, proj_ref, ymix_ref, sc_ref, ss_ref, stage_ref, perm_ref, tri, xp_ref, xp2_ref, cw_ref, cb_ref,
               dtb_ref, a_aux, drep_ref, sn_ref):
    rows = pl.ds(r0, CHUNK)
    z = proj_ref[rows, C_Z:C_Z + SSD_INNER]
    xbc = proj_ref[rows, C_XBC:C_XBC + SSD_XBC]
    aux = proj_ref[rows, C_AUX:C_AUX + V7X_LANES]
    conv = _causal_conv_interleaved(stage_ref, xbc, sc_ref[s], cw_ref, cb_ref, SSD_CONV)
    xc = _deinterleave(perm_ref, [_silu(c) for c in conv])
    sc_ref[s] = xbc[CHUNK - V7X_SUBLANES:]
    xs = xc[:, :SSD_INNER]
    gw = SSD_STATE
    lane = _lane_iota((CHUNK, V7X_LANES))
    is_dt = (lane >= AUX_DT) & (lane < AUX_DT + SSD_HEADS)
    dt_aux = jnp.where(is_dt, _softplus(aux + dtb_ref[...]), 0.0)
    cum_aux = _mm_exact_lhs(tri, dt_aux * a_aux)
    heads_per_group = SSD_HEADS // SSD_GROUPS
    group_width = heads_per_group * SSD_HEAD_DIM
    bgs, sc2s, states, y_inters = [], [], [], []
    for grp in range(SSD_GROUPS):
        bg = xc[:, SSD_INNER + grp * gw:SSD_INNER + (grp + 1) * gw].astype(BF16)
        cg = xc[:, SSD_INNER + SSD_GROUPS * gw + grp * gw:SSD_INNER + SSD_GROUPS * gw + (grp + 1) * gw].astype(BF16)
        state = ss_ref[s, grp * group_width:(grp + 1) * group_width, :]
        bgs.append(bg)
        sc2s.append(_mm_nt(cg, jnp.concatenate([bg, bg], axis=0)))
        states.append(state)
        y_inters.append(_mm_nt(cg, state.astype(BF16)))
    yield
    e = _mm_exact_rhs(jnp.concatenate([cum_aux, dt_aux], axis=0), xp_ref[...])
    last_rows = _mm_exact_rhs(cum_aux[CHUNK - V7X_SUBLANES:], xp2_ref[...])
    yield
    e_cum, e_dt = e[:CHUNK], e[CHUNK:]
    last_e = e_cum[CHUNK - 1:]
    xw = xs * (jnp.exp(last_e - e_cum) * e_dt)
    dec_rows = jnp.exp(last_rows[V7X_SUBLANES - 1:])
    shp = (CHUNK, V7X_LANES)
    col_t = _lane_iota(shp) & (CHUNK - 1)
    diag = _row_iota(shp) == col_t
    causal = _row_iota(shp) >= col_t
    low = _lane_iota(shp) < SSD_HEAD_DIM
    ys, dss = [], []
    for grp in range(SSD_GROUPS):
        glanes = slice(grp * group_width, (grp + 1) * group_width)
        dss.append(_mm(xw[:, glanes].T.astype(BF16), bgs[grp]))
        y_inter = y_inters[grp] * jnp.exp(e_cum[:, glanes])
        for pair in range(heads_per_group // 2):
            lo = grp * group_width + pair * V7X_LANES
            pl_ = slice(lo, lo + V7X_LANES)
            ec = e_cum[:, pl_]
            rc = jnp.sum(jnp.where(diag, ec, 0.0), axis=0, keepdims=True)
            rd = jnp.sum(jnp.where(diag, e_dt[:, pl_], 0.0), axis=0, keepdims=True)
            decay = jnp.where(causal, jnp.exp(jnp.where(causal, ec - rc, 0.0)), 0.0)
            m = (sc2s[grp] * decay * rd).astype(BF16)
            xq = xs[:, pl_]
            rhs = jnp.concatenate([jnp.where(low, xq, 0.0), jnp.where(low, 0.0, xq)], axis=0).astype(BF16)
            ys.append(_mm(m, rhs) + y_inter[:, pair * V7X_LANES:(pair + 1) * V7X_LANES])
    yield
    for grp in range(SSD_GROUPS):
        for hh in range(heads_per_group):
            h = grp * heads_per_group + hh
            rr = slice(hh * SSD_HEAD_DIM, (hh + 1) * SSD_HEAD_DIM)
            dec = dec_rows[:, h * V7X_LANES:(h + 1) * V7X_LANES]
            ss_ref[s, h * SSD_HEAD_DIM:(h + 1) * SSD_HEAD_DIM, :] = dec * states[grp][rr] + dss[grp][rr]
    y = jnp.concatenate(ys, axis=1) + drep_ref[...] * xs
    y = _rms(y * _silu(z), sn_ref[...])
    ymix_ref[rows, D_LRU + GLA_VAL:D_MIX] = y.astype(BF16)


def _mixer_kernel(x_ref, lc_in, lh_in, gs_in, sc_in, ss_in,
                  nm_ref, win_ref, lcw_ref, lcb_ref, wr_ref, br_ref, wi_ref, bi_ref, lam_ref,
                  wlr_ref, blr_ref, gn_ref, scw_ref, scb_ref, dtb_ref, alog_ref, drep_ref, sn_ref, wout_ref,
                  tri_ref, xp_ref, xp2_ref,
                  o_ref, lc_ref, lh_ref, gs_ref, sc_ref, ss_ref,
                  xn_ref, proj_ref, ymix_ref, lstage_ref, lperm_ref, sstage_ref, sperm_ref, *, n_seq, n_chunk):
    @pl.when(pl.program_id(1) == 0)
    def _():
        lc_ref[...] = lc_in[...]
        lh_ref[...] = lh_in[...]
        gs_ref[...] = gs_in[...]
        sc_ref[...] = sc_in[...]
        ss_ref[...] = ss_in[...]

    xn_ref[...] = _rms(x_ref[...], nm_ref[...]).astype(BF16)
    for lo in range(0, D_IN_PAD, PROJ_TILE):
        hi = min(lo + PROJ_TILE, D_IN_PAD)
        proj_ref[:, lo:hi] = _mm(xn_ref[...], win_ref[:, lo:hi])

    c_lam = -LRU_C * _softplus(-lam_ref[...])
    lane = _lane_iota((1, V7X_LANES))
    a_aux = jnp.where((lane >= AUX_DT) & (lane < AUX_DT + SSD_HEADS), -jnp.exp(alog_ref[...]), 0.0)
    tri = tri_ref[...]

    def chunk(c, carry):
        r0 = pl.multiple_of(c * CHUNK, CHUNK)
        s = 0 if n_seq == 1 else (c if n_chunk == 1 else c // n_chunk)
        _trace_round_robin(
            _gla_chunk(r0, s, proj_ref, ymix_ref, gs_ref, tri, wlr_ref, blr_ref, gn_ref),
            _ssd_chunk(r0, s, proj_ref, ymix_ref, sc_ref, ss_ref, sstage_ref, sperm_ref, tri, xp_ref, xp2_ref, scw_ref,
                       scb_ref, dtb_ref, a_aux, drep_ref, sn_ref),
            _lru_chunk(r0, s, proj_ref, ymix_ref, lc_ref, lh_ref, lstage_ref, lperm_ref, lcw_ref, lcb_ref, wr_ref,
                       br_ref, wi_ref, bi_ref, c_lam))
        return carry

    lax.fori_loop(0, n_seq * n_chunk, chunk, 0)
    o_ref[...] = x_ref[...] + _mm(ymix_ref[...], wout_ref[...])


def _ffn_kernel(x_ref, p_ref, fc_in, nf_ref, wg_ref, wu_ref, cw_ref, cb_ref, wd_ref, npl_ref, pwg_ref, pwp_ref,
                nfin_ref, o_ref, fc_ref, xn_ref, h_ref, *, n_seq, final):
    @pl.when(pl.program_id(1) == 0)
    def _():
        fc_ref[...] = fc_in[...]

    seq_rows = ROWS // n_seq
    xn_ref[...] = _rms(x_ref[...], nf_ref[...]).astype(BF16)
    for lo in range(0, D_FF, FF_TILE):
        cols = slice(lo, lo + FF_TILE)
        g = _mm(xn_ref[...], wg_ref[:, cols])
        u = _mm(xn_ref[...], wu_ref[:, cols])
        pieces = []
        for s in range(n_seq):
            gs = g[s * seq_rows:(s + 1) * seq_rows]
            p = jnp.concatenate([fc_ref[s, :, cols], gs], axis=0)
            c = cb_ref[:, cols] + cw_ref[2:3, cols] * gs
            c = c + cw_ref[1:2, cols] * pltpu.roll(p, 1, 0)[V7X_SUBLANES:]
            c = c + cw_ref[0:1, cols] * pltpu.roll(p, 2, 0)[V7X_SUBLANES:]
            fc_ref[s, :, cols] = gs[seq_rows - V7X_SUBLANES:]
            pieces.append(c)
        gc = pieces[0] if n_seq == 1 else jnp.concatenate(pieces, axis=0)
        h_ref[:, cols] = (_gelu_tanh(gc) * u).astype(BF16)
    o_ref[...] = x_ref[...] + _mm(h_ref[...], wd_ref[...])
    gate = jax.nn.sigmoid(_mm(_rms(o_ref[...], npl_ref[...]).astype(BF16), pwg_ref[...]))
    y = o_ref[...] + gate * _mm(p_ref[...].astype(BF16), pwp_ref[...])
    if final:
        y = _rms(y, nfin_ref[...])
    o_ref[...] = y


def _vmem_limit(n_bytes):
    return int(min(n_bytes + (8 << 20), V7X_VMEM_BYTES - (4 << 20)))


def _nbytes(shape, dtype):
    return int(np.prod(shape)) * jnp.dtype(dtype).itemsize


def _weight_spec(arr, layer):
    return pl.BlockSpec((None,) + arr.shape[1:], lambda o, t: (layer, 0, 0), pipeline_mode=pl.Buffered(1))


def _const_spec(arr):
    return pl.BlockSpec(arr.shape, lambda o, t: (0, 0), pipeline_mode=pl.Buffered(1))


def _mixer_call(x, states, w, consts, layer, n_seq, n_chunk):
    n_rows = x.shape[0]
    n_batch = states[0].shape[0]
    n_outer = n_batch // n_seq
    n_t = n_rows // (ROWS * n_outer)
    assert n_seq * n_chunk * CHUNK == ROWS and n_outer * n_t * ROWS == n_rows
    row_spec = pl.BlockSpec((ROWS, D_MODEL), lambda o, t: (o * n_t + t, 0))
    state_specs = [pl.BlockSpec((n_seq,) + st.shape[1:], lambda o, t: (o, 0, 0)) for st in states]
    wnames = ("norm_mix", "w_in", "lru_conv_w", "lru_conv_b", "lru_wr", "lru_b_r", "lru_wi", "lru_b_i", "lru_lambda",
              "gla_w_lr", "gla_b_lr", "gla_norm", "ssd_conv_w", "ssd_conv_b", "ssd_dtb", "ssd_alog", "ssd_drep",
              "ssd_norm", "w_out")
    ws = [w[n] for n in wnames]
    w_specs = []
    for a in ws:
        if a.ndim == 4:
            w_specs.append(pl.BlockSpec((None,) + a.shape[1:], lambda o, t: (layer, 0, 0, 0),
                                        pipeline_mode=pl.Buffered(1)))
        else:
            w_specs.append(_weight_spec(a, layer))
    c_specs = [_const_spec(c) for c in consts]
    lru_tiles, ssd_tiles = D_LRU // V7X_LANES, SSD_XBC // V7X_LANES
    scratch = [pltpu.VMEM((ROWS, D_MODEL), BF16), pltpu.VMEM((ROWS, D_IN_PAD), F32), pltpu.VMEM((ROWS, D_MIX), BF16),
               pltpu.VMEM((lru_tiles, V7X_SUBLANES + CHUNK, V7X_LANES), F32),
               pltpu.VMEM((lru_tiles, CHUNK, V7X_LANES), F32),
               pltpu.VMEM((ssd_tiles, V7X_SUBLANES + CHUNK, V7X_LANES), F32),
               pltpu.VMEM((ssd_tiles, CHUNK, V7X_LANES), F32)]
    est = (4 * _nbytes((ROWS, D_MODEL), F32)
           + 4 * sum(_nbytes((n_seq,) + st.shape[1:], F32) for st in states)
           + sum(_nbytes(a.shape[1:], a.dtype) for a in ws) + sum(_nbytes(c.shape, c.dtype) for c in consts)
           + _nbytes((ROWS, D_MODEL), BF16) + _nbytes((ROWS, D_IN_PAD), F32) + _nbytes((ROWS, D_MIX), BF16))
    out = pl.pallas_call(
        functools.partial(_mixer_kernel, n_seq=n_seq, n_chunk=n_chunk),
        grid=(n_outer, n_t),
        in_specs=[row_spec] + state_specs + w_specs + c_specs,
        out_specs=[row_spec] + state_specs,
        out_shape=[jax.ShapeDtypeStruct(x.shape, F32)] + [jax.ShapeDtypeStruct(st.shape, F32) for st in states],
        scratch_shapes=scratch,
        compiler_params=pltpu.CompilerParams(dimension_semantics=("arbitrary", "arbitrary"),
                                             vmem_limit_bytes=_vmem_limit(est)),
        name=f"mixer_l{layer}_s{n_seq}",
    )(x, *states, *ws, *consts)
    return out[0], out[1:]


def _ffn_call(x, p, fc, w, layer, n_seq, final):
    n_rows = x.shape[0]
    n_outer = fc.shape[0] // n_seq
    n_t = n_rows // (ROWS * n_outer)
    assert n_outer * n_t * ROWS == n_rows
    row_spec = pl.BlockSpec((ROWS, D_MODEL), lambda o, t: (o * n_t + t, 0))
    p_spec = pl.BlockSpec((None, ROWS, D_PLE), lambda o, t: (layer, o * n_t + t, 0))
    fc_spec = pl.BlockSpec((n_seq,) + fc.shape[1:], lambda o, t: (o, 0, 0))
    wnames = ("norm_ffn", "ffn_w_gate", "ffn_w_up", "ffn_conv_w", "ffn_conv_b", "ffn_w_down", "norm_ple",
              "ple_w_gate", "ple_w_proj")
    ws = [w[n] for n in wnames]
    nfin = w["norm_final"]
    scratch = [pltpu.VMEM((ROWS, D_MODEL), BF16), pltpu.VMEM((ROWS, D_FF), BF16)]
    est = (4 * _nbytes((ROWS, D_MODEL), F32) + 2 * _nbytes((ROWS, D_PLE), F32)
           + 4 * _nbytes((n_seq,) + fc.shape[1:], F32)
           + sum(_nbytes(a.shape[1:], a.dtype) for a in ws)
           + _nbytes((ROWS, D_MODEL), BF16) + _nbytes((ROWS, D_FF), BF16)
           + 4 * _nbytes((ROWS, FF_TILE), F32))
    out = pl.pallas_call(
        functools.partial(_ffn_kernel, n_seq=n_seq, final=final),
        grid=(n_outer, n_t),
        in_specs=[row_spec, p_spec, fc_spec] + [_weight_spec(a, layer) for a in ws] + [_const_spec(nfin)],
        out_specs=[row_spec, fc_spec],
        out_shape=[jax.ShapeDtypeStruct(x.shape, F32), jax.ShapeDtypeStruct(fc.shape, F32)],
        scratch_shapes=scratch,
        compiler_params=pltpu.CompilerParams(dimension_semantics=("arbitrary", "arbitrary"),
                                             vmem_limit_bytes=_vmem_limit(est)),
        name=f"ffn_l{layer}_s{n_seq}",
    )(x, p, fc, *ws, nfin)
    return out[0], out[1]


def _block_diag_gate(w):
    d = w.shape[0]
    w5 = w.reshape(d, 2, 4, 64, 64)
    out = jnp.einsum("dghij,hk->dghikj", w5, jnp.eye(4, dtype=w.dtype))
    return out.reshape(d, 2, 256, 256).astype(BF16)


def _prepare_weights(norm_mix, w_in, lru_conv_w, lru_conv_b, lru_w_r, lru_b_r, lru_w_i, lru_b_i, lru_lambda,
                     gla_w_lr, gla_b_lr, gla_norm, ssd_conv_w, ssd_conv_b, ssd_dt_bias, ssd_a_log, ssd_d, ssd_norm,
                     w_out, norm_ffn, ffn_w_gate, ffn_w_up, ffn_conv_w, ffn_conv_b, ffn_w_down, norm_ple,
                     ple_w_gate, ple_w_proj, norm_final):
    def row(v):
        return v.reshape(DEPTH, 1, -1)

    lr0 = 2 * D_LRU + 2 * GLA_KEY + 2 * GLA_VAL
    dt0 = lr0 + GLA_RANK + SSD_INNER + SSD_XBC
    w_in_r = jnp.concatenate(
        [w_in[:, :, :lr0], w_in[:, :, lr0 + GLA_RANK:dt0], w_in[:, :, lr0:lr0 + GLA_RANK], w_in[:, :, dt0:],
         jnp.zeros((DEPTH, D_MODEL, V7X_LANES - GLA_RANK - SSD_HEADS), w_in.dtype)], axis=2).astype(BF16)
    pad_aux = lambda v: jnp.pad(v, ((0, 0), (AUX_DT, V7X_LANES - AUX_DT - SSD_HEADS)))
    return {
        "norm_mix": row(norm_mix), "w_in": w_in_r,
        "lru_conv_w": lru_conv_w, "lru_conv_b": row(lru_conv_b),
        "lru_wr": _block_diag_gate(lru_w_r), "lru_b_r": row(lru_b_r),
        "lru_wi": _block_diag_gate(lru_w_i), "lru_b_i": row(lru_b_i), "lru_lambda": row(lru_lambda),
        "gla_w_lr": jnp.pad(gla_w_lr, ((0, 0), (0, V7X_LANES - GLA_RANK), (0, 0))).astype(BF16),
        "gla_b_lr": row(gla_b_lr), "gla_norm": row(gla_norm),
        "ssd_conv_w": ssd_conv_w, "ssd_conv_b": row(ssd_conv_b),
        "ssd_dtb": row(pad_aux(ssd_dt_bias)), "ssd_alog": row(pad_aux(ssd_a_log)),
        "ssd_drep": row(jnp.repeat(ssd_d, SSD_HEAD_DIM, axis=1)), "ssd_norm": row(ssd_norm),
        "w_out": w_out.astype(BF16),
        "norm_ffn": row(norm_ffn), "ffn_w_gate": ffn_w_gate.astype(BF16), "ffn_w_up": ffn_w_up.astype(BF16),
        "ffn_conv_w": ffn_conv_w, "ffn_conv_b": row(ffn_conv_b), "ffn_w_down": ffn_w_down.astype(BF16),
        "norm_ple": row(norm_ple), "ple_w_gate": ple_w_gate.astype(BF16), "ple_w_proj": ple_w_proj.astype(BF16),
        "norm_final": norm_final.reshape(1, -1),
    }


def _constants():
    i = np.arange(CHUNK)
    tri = (i[None, :] <= i[:, None]).astype(np.float32)
    expand = np.zeros((V7X_LANES, SSD_INNER), np.float32)
    expand2 = np.zeros((V7X_LANES, SSD_HEADS * V7X_LANES), np.float32)
    for h in range(SSD_HEADS):
        expand[AUX_DT + h, h * SSD_HEAD_DIM:(h + 1) * SSD_HEAD_DIM] = 1.0
        expand2[AUX_DT + h, h * V7X_LANES:(h + 1) * V7X_LANES] = 1.0
    return [jnp.asarray(tri, BF16), jnp.asarray(expand, BF16), jnp.asarray(expand2, BF16)]


def _pad_rows_front(buf):
    return jnp.pad(buf, ((0, 0), (0, 0), (V7X_SUBLANES - buf.shape[2], 0), (0, 0)))


def _run_trunk(x, p, lru_conv, lru_h, gla_s, ssd_conv, ssd_s, ffn_conv, w, consts, n_seq, n_chunk):
    b, t, _ = x.shape
    xr = x.reshape(b * t, D_MODEL)
    pr = p.reshape(DEPTH, b * t, D_PLE)
    lc = _pad_rows_front(lru_conv)
    lh = lru_h.reshape(DEPTH, b, 1, D_LRU)
    gs = gla_s.reshape(DEPTH, b, GLA_KEY, GLA_DV)
    sc = _pad_rows_front(ssd_conv)
    ss = ssd_s.reshape(DEPTH, b, SSD_INNER, SSD_STATE)
    fc = _pad_rows_front(ffn_conv)
    new = []
    for layer in range(DEPTH):
        xr, st = _mixer_call(xr, [lc[layer], lh[layer], gs[layer], sc[layer], ss[layer]], w, consts, layer,
                             n_seq, n_chunk)
        xr, fcn = _ffn_call(xr, pr, fc[layer], w, layer, n_seq, layer == DEPTH - 1)
        new.append(list(st) + [fcn])
    stack = lambda k: jnp.stack([n[k] for n in new])
    return (xr.reshape(b, t, D_MODEL),
            stack(0)[:, :, V7X_SUBLANES - (LRU_CONV - 1):],
            stack(1).reshape(DEPTH, b, D_LRU),
            stack(2).reshape(DEPTH, b, GLA_HEADS, GLA_DK, GLA_DV),
            stack(3)[:, :, V7X_SUBLANES - (SSD_CONV - 1):],
            stack(4).reshape(DEPTH, b, SSD_HEADS, SSD_HEAD_DIM, SSD_STATE),
            stack(5)[:, :, V7X_SUBLANES - (FFN_CONV - 1):])


def kernel(x_prompt, x_sample, state_lru_conv, state_lru_h, state_gla, state_ssd_conv, state_ssd, state_ffn_conv, p_prompt, p_sample, norm_mix, w_in, lru_conv_w, lru_conv_b, lru_w_r, lru_b_r, lru_w_i, lru_b_i, lru_lambda, gla_w_lr, gla_b_lr, gla_norm, ssd_conv_w, ssd_conv_b, ssd_dt_bias, ssd_a_log, ssd_d, ssd_norm, w_out, norm_ffn, ffn_w_gate, ffn_w_up, ffn_conv_w, ffn_conv_b, ffn_w_down, norm_ple, ple_w_gate, ple_w_proj, norm_final):
    w = _prepare_weights(norm_mix, w_in, lru_conv_w, lru_conv_b, lru_w_r, lru_b_r, lru_w_i, lru_b_i, lru_lambda,
                         gla_w_lr, gla_b_lr, gla_norm, ssd_conv_w, ssd_conv_b, ssd_dt_bias, ssd_a_log, ssd_d,
                         ssd_norm, w_out, norm_ffn, ffn_w_gate, ffn_w_up, ffn_conv_w, ffn_conv_b, ffn_w_down,
                         norm_ple, ple_w_gate, ple_w_proj, norm_final)
    consts = _constants()
    bp = x_prompt.shape[0]
    dt = x_prompt.dtype
    zeros = lambda *shape: jnp.zeros((DEPTH, bp) + shape, dt)
    prompt = _run_trunk(x_prompt, p_prompt,
                        zeros(LRU_CONV - 1, D_LRU), zeros(D_LRU), zeros(GLA_HEADS, GLA_DK, GLA_DV),
                        zeros(SSD_CONV - 1, SSD_XBC), zeros(SSD_HEADS, SSD_HEAD_DIM, SSD_STATE),
                        zeros(FFN_CONV - 1, D_FF), w, consts, n_seq=1, n_chunk=ROWS // CHUNK)
    bs, ts, _ = x_sample.shape
    sample = _run_trunk(x_sample, p_sample, state_lru_conv, state_lru_h, state_gla, state_ssd_conv, state_ssd,
                        state_ffn_conv, w, consts, n_seq=ROWS // ts, n_chunk=ts // CHUNK)
    return (prompt[0], sample[0]) + tuple(prompt[1:]) + tuple(sample[1:])
```
